```python
import jax, jax.numpy as jnp
from jax import lax
import numpy as np


D_MODEL = 1024
BATCH = 8
SEQ = 4096
DEPTH = 1
DEC_BATCH = 32
DEC_SEQ = 1
PAST_LEN = 16384
PAGE_SIZE = 128

ATTN_WIDTH = D_MODEL // 2
POOL_WIDTH = D_MODEL - ATTN_WIDTH
N_HEADS = 8
HEAD_DIM = ATTN_WIDTH // N_HEADS
POOL_WINDOWS = (2, 4, 8, 16)
N_POOL_GROUPS = len(POOL_WINDOWS)
POOL_GROUP_DIM = POOL_WIDTH // N_POOL_GROUPS
POOL_STATE_LEN = max(POOL_WINDOWS) - 1
D_FF = -(-8 * D_MODEL // 768) * 256
IN_COLS = 3 * ATTN_WIDTH + N_HEADS + POOL_WIDTH
QBLOCK = 128
DEEPNORM_ALPHA = (2.0 * DEPTH) ** 0.25
DEEPNORM_BETA = (8.0 * DEPTH) ** -0.25
LN_EPS = 1e-5
NEG_INF = -1e30

kernel_name = "hymba_fox_poolformer_deepnorm_step"


def layer_norm(x, g, b):
    xf = x.astype(jnp.float32)
    mu = jnp.mean(xf, axis=-1, keepdims=True)
    var = jnp.mean(jnp.square(xf - mu), axis=-1, keepdims=True)
    return ((xf - mu) * lax.rsqrt(var + LN_EPS) * g + b).astype(x.dtype)


def project(x, w_in, b_f):
    h = jnp.einsum('bsd,dc->bsc', x, w_in)
    B, S = h.shape[:2]
    A = ATTN_WIDTH
    q = h[..., :A].reshape(B, S, N_HEADS, HEAD_DIM)
    k = h[..., A:2 * A].reshape(B, S, N_HEADS, HEAD_DIM)
    v = h[..., 2 * A:3 * A].reshape(B, S, N_HEADS, HEAD_DIM)
    logf = jax.nn.log_sigmoid((h[..., 3 * A:3 * A + N_HEADS] + b_f).astype(jnp.float32))
    u = h[..., 3 * A + N_HEADS:]
    return q, k, v, logf, u


def pool_mix(u_ext, start_pos, w_pool, pool_scale):
    B, L, _ = u_ext.shape
    n_new = L - POOL_STATE_LEN
    c = jnp.pad(jnp.cumsum(u_ext.astype(jnp.float32), axis=1), ((0, 0), (1, 0), (0, 0)))
    end = c[:, POOL_STATE_LEN + 1:]
    pos = start_pos + jnp.arange(n_new)
    means = []
    for g, w in enumerate(POOL_WINDOWS):
        lo, hi = g * POOL_GROUP_DIM, (g + 1) * POOL_GROUP_DIM
        begin = c[:, POOL_STATE_LEN + 1 - w:POOL_STATE_LEN + 1 - w + n_new, lo:hi]
        cnt = jnp.minimum(pos + 1, w).astype(jnp.float32)[None, :, None]
        means.append((end[..., lo:hi] - begin) / cnt)
    mean = jnp.concatenate(means, axis=-1)
    d = (mean - u_ext[:, POOL_STATE_LEN:].astype(jnp.float32)).astype(u_ext.dtype)
    d = d.reshape(B, n_new, N_POOL_GROUPS, POOL_GROUP_DIM)
    out = jnp.einsum('bsgc,gcd->bsgd', d, w_pool).reshape(B, n_new, POOL_WIDTH)
    return out * pool_scale


def fox_prompt(q, k, v, logf):
    B, S = q.shape[:2]
    Fh = jnp.cumsum(logf, axis=1).transpose(0, 2, 1)
    scale = HEAD_DIM ** -0.5
    kpos = jnp.arange(S)

    def block(i):
        q_blk = lax.dynamic_slice_in_dim(q, i * QBLOCK, QBLOCK, axis=1)
        F_blk = lax.dynamic_slice_in_dim(Fh, i * QBLOCK, QBLOCK, axis=2)
        s = jnp.einsum('bqhd,bkhd->bhqk', q_blk, k, preferred_element_type=jnp.float32) * scale
        s = s + (F_blk[..., :, None] - Fh[..., None, :])
        qpos = i * QBLOCK + jnp.arange(QBLOCK)
        s = jnp.where(kpos[None, :] <= qpos[:, None], s, NEG_INF)
        p = jax.nn.softmax(s, axis=-1)
        return jnp.einsum('bhqk,bkhd->bqhd', p.astype(v.dtype), v)

    out = lax.map(block, jnp.arange(S // QBLOCK))
    return out.transpose(1, 0, 2, 3, 4).reshape(B, S, ATTN_WIDTH)


def fox_sample(q, k_new, v_new, logf_new, cache_k, cache_v, cache_logf, page_table):
    DB, DS = q.shape[:2]
    P = page_table.shape[1] * PAGE_SIZE
    k_past = cache_k[page_table].reshape(DB, P, N_HEADS, HEAD_DIM)
    v_past = cache_v[page_table].reshape(DB, P, N_HEADS, HEAD_DIM)
    f_past = cache_logf[page_table].reshape(DB, P, N_HEADS).astype(jnp.float32)
    k_all = jnp.concatenate([k_past, k_new.astype(k_past.dtype)], axis=1)
    v_all = jnp.concatenate([v_past, v_new.astype(v_past.dtype)], axis=1)
    Fh = jnp.cumsum(jnp.concatenate([f_past, logf_new], axis=1), axis=1).transpose(0, 2, 1)
    s = jnp.einsum('bqhd,bkhd->bhqk', q, k_all, preferred_element_type=jnp.float32) * HEAD_DIM ** -0.5
    s = s + (Fh[..., P:, None] - Fh[..., None, :])
    kpos = jnp.arange(P + DS)
    qpos = P + jnp.arange(DS)
    s = jnp.where(kpos[None, :] <= qpos[:, None], s, NEG_INF)
    p = jax.nn.softmax(s, axis=-1)
    out = jnp.einsum('bhqk,bkhd->bqhd', p.astype(v_all.dtype), v_all)
    return out.reshape(DB, DS, ATTN_WIDTH)


def finish(x, attn, pool, w_o, ln1_g, ln1_b, w_gate, w_up, w_down, ln2_g, ln2_b):
    mix = jnp.einsum('bsc,cd->bsd', jnp.concatenate([attn.astype(x.dtype), pool.astype(x.dtype)], axis=-1), w_o)
    x1 = layer_norm(DEEPNORM_ALPHA * x + mix, ln1_g, ln1_b)
    hid = jax.nn.silu(jnp.einsum('bsd,df->bsf', x1, w_gate)) * jnp.einsum('bsd,df->bsf', x1, w_up)
    ffn = jnp.einsum('bsf,fd->bsd', hid, w_down)
    return layer_norm(DEEPNORM_ALPHA * x1 + ffn, ln2_g, ln2_b)


def setup_inputs(seed: int = 0) -> dict:
    key = jax.random.key(seed)
    ks = jax.random.split(key, 24)
    f32 = jnp.float32
    n_pages = PAST_LEN // PAGE_SIZE
    n_used = DEC_BATCH * n_pages
    n_pool = n_used + n_used // 4
    x_prompt = jax.random.normal(ks[0], (BATCH, SEQ, D_MODEL), f32)
    x_sample = jax.random.normal(ks[1], (DEC_BATCH, DEC_SEQ, D_MODEL), f32)
    cache_k = jax.random.normal(ks[2], (DEPTH, n_pool, PAGE_SIZE, N_HEADS, HEAD_DIM), f32)
    cache_v = DEEPNORM_BETA * jax.random.normal(ks[3], (DEPTH, n_pool, PAGE_SIZE, N_HEADS, HEAD_DIM), f32)
    cache_logf = jax.nn.log_sigmoid(3.0 + 0.5 * jax.random.normal(ks[4], (DEPTH, n_pool, PAGE_SIZE, N_HEADS), f32))
    state_pool = jax.random.normal(ks[5], (DEPTH, DEC_BATCH, POOL_STATE_LEN, POOL_WIDTH), f32)
    page_table = jax.random.permutation(ks[6], n_pool)[:n_used].reshape(DEC_BATCH, n_pages).astype(jnp.int32)
    col_scale = jnp.concatenate([
        jnp.ones((2 * ATTN_WIDTH,), f32),
        jnp.full((ATTN_WIDTH,), DEEPNORM_BETA, f32),
        jnp.full((N_HEADS,), 0.1, f32),
        jnp.ones((POOL_WIDTH,), f32)])
    w_in = jax.random.normal(ks[7], (DEPTH, D_MODEL, IN_COLS), f32) * D_MODEL ** -0.5 * col_scale
    b_f = 3.0 + 0.5 * jax.random.normal(ks[8], (DEPTH, N_HEADS), f32)
    w_pool = jax.random.normal(ks[9], (DEPTH, N_POOL_GROUPS, POOL_GROUP_DIM, POOL_GROUP_DIM), f32) * POOL_GROUP_DIM ** -0.5
    pool_scale = 1.0 + 0.1 * jax.random.normal(ks[10], (DEPTH, POOL_WIDTH), f32)
    w_o = jax.random.normal(ks[11], (DEPTH, D_MODEL, D_MODEL), f32) * D_MODEL ** -0.5 * DEEPNORM_BETA
    ln1_g = 1.0 + 0.1 * jax.random.normal(ks[12], (DEPTH, D_MODEL), f32)
    ln1_b = 0.02 * jax.random.normal(ks[13], (DEPTH, D_MODEL), f32)
    w_gate = jax.random.normal(ks[14], (DEPTH, D_MODEL, D_FF), f32) * D_MODEL ** -0.5
    w_up = jax.random.normal(ks[15], (DEPTH, D_MODEL, D_FF), f32) * D_MODEL ** -0.5
    w_down = jax.random.normal(ks[16], (DEPTH, D_FF, D_MODEL), f32) * D_FF ** -0.5 * DEEPNORM_BETA
    ln2_g = 1.0 + 0.1 * jax.random.normal(ks[17], (DEPTH, D_MODEL), f32)
    ln2_b = 0.02 * jax.random.normal(ks[18], (DEPTH, D_MODEL), f32)
    return {"x_prompt": x_prompt, "x_sample": x_sample, "cache_k": cache_k, "cache_v": cache_v,
            "cache_logf": cache_logf, "state_pool": state_pool, "page_table": page_table,
            "w_in": w_in, "b_f": b_f, "w_pool": w_pool, "pool_scale": pool_scale, "w_o": w_o,
            "ln1_g": ln1_g, "ln1_b": ln1_b, "w_gate": w_gate, "w_up": w_up, "w_down": w_down,
            "ln2_g": ln2_g, "ln2_b": ln2_b}


def reference(x_prompt, x_sample, cache_k, cache_v, cache_logf, state_pool, page_table,
              w_in, b_f, w_pool, pool_scale, w_o, ln1_g, ln1_b, w_gate, w_up, w_down, ln2_g, ln2_b):
    xp, xs = x_prompt, x_sample
    kp, vp, fp, pp, ksm, vsm, fsm, psm = [], [], [], [], [], [], [], []
    for l in range(DEPTH):
        q, k, v, logf, u = project(xp, w_in[l], b_f[l])
        attn = fox_prompt(q, k, v, logf)
        u_ext = jnp.pad(u, ((0, 0), (POOL_STATE_LEN, 0), (0, 0)))
        pool = pool_mix(u_ext, 0, w_pool[l], pool_scale[l])
        kp.append(k)
        vp.append(v)
        fp.append(logf.astype(xp.dtype))
        pp.append(u_ext[:, -POOL_STATE_LEN:])
        xp = finish(xp, attn, pool, w_o[l], ln1_g[l], ln1_b[l], w_gate[l], w_up[l], w_down[l], ln2_g[l], ln2_b[l])
        q, k, v, logf, u = project(xs, w_in[l], b_f[l])
        attn = fox_sample(q, k, v, logf, cache_k[l], cache_v[l], cache_logf[l], page_table)
        u_ext = jnp.concatenate([state_pool[l].astype(u.dtype), u], axis=1)
        pool = pool_mix(u_ext, PAST_LEN, w_pool[l], pool_scale[l])
        ksm.append(k)
        vsm.append(v)
        fsm.append(logf.astype(xs.dtype))
        psm.append(u_ext[:, -POOL_STATE_LEN:])
        xs = finish(xs, attn, pool, w_o[l], ln1_g[l], ln1_b[l], w_gate[l], w_up[l], w_down[l], ln2_g[l], ln2_b[l])
    return (xp, xs, jnp.stack(kp), jnp.stack(vp), jnp.stack(fp), jnp.stack(pp),
            jnp.stack(ksm), jnp.stack(vsm), jnp.stack(fsm), jnp.stack(psm))
```

```python
import functools

import jax
import jax.numpy as jnp
from jax import lax
from jax.experimental import pallas as pl
from jax.experimental.pallas import tpu as pltpu

N_HEADS = 8
POOL_WINDOWS = (2, 4, 8, 16)
POOL_STATE_LEN = max(POOL_WINDOWS) - 1
LN_EPS = 1e-5
NEG_INF = -1e30

LANES = 128
CARRY_ROWS = 16
VMEM_LIMIT = 56 * 1024 * 1024

BF16 = jnp.bfloat16
F32 = jnp.float32


def _dot(a, b):
    return jnp.dot(a, b, preferred_element_type=F32)


def _dot_nt(a, b):
    return lax.dot_general(a, b, (((1,), (1,)), ((), ())), preferred_element_type=F32)


def _log_sigmoid(z):
    return jnp.minimum(z, 0.0) - jnp.log1p(jnp.exp(-jnp.abs(z)))


def _lane_cumsum(x):
    n = x.shape[-1]
    lane = lax.broadcasted_iota(jnp.int32, x.shape, x.ndim - 1)
    shift = 1
    while shift < n:
        rolled = pltpu.roll(x, shift, axis=x.ndim - 1)
        x = x + jnp.where(lane >= shift, rolled, 0.0)
        shift *= 2
    return x


def _layer_norm(x, g, b):
    mu = jnp.mean(x, axis=-1, keepdims=True)
    xc = x - mu
    var = jnp.mean(xc * xc, axis=-1, keepdims=True)
    return xc * lax.rsqrt(var + LN_EPS) * g + b


def _proj_kernel(x_ref, w_ref, wf_ref, bf_ref, wp_ref, ps_ref,
                 k32_ref, v32_ref, qb_ref, kb_ref, vb_ref, lft_ref, fr_ref, pool_ref, ulast_ref,
                 ubuf, fcarry, *, tm, aw, pw, scale):
    si = pl.program_id(1)
    n_s = pl.num_programs(1)

    @pl.when(si == 0)
    def _():
        ubuf[0:CARRY_ROWS, :] = jnp.zeros((CARRY_ROWS, pw), F32)
        fcarry[...] = jnp.zeros(fcarry.shape, F32)

    xb = x_ref[...].astype(BF16)
    q = _dot(xb, w_ref[:, 0:aw])
    qb_ref[...] = (q * scale).astype(BF16)
    k = _dot(xb, w_ref[:, aw:2 * aw])
    k32_ref[...] = k
    kb_ref[...] = k.astype(BF16)
    v = _dot(xb, w_ref[:, 2 * aw:3 * aw])
    v32_ref[...] = v
    vb_ref[...] = v.astype(BF16)

    z = _dot(xb, wf_ref[...]) + bf_ref[...]
    lft = jnp.transpose(_log_sigmoid(z))[0:N_HEADS, :]
    lft_ref[...] = lft
    fr = _lane_cumsum(lft) + fcarry[:, 0:1]
    fr_ref[...] = fr
    fcarry[...] = jnp.broadcast_to(fr[:, tm - 1:tm], fcarry.shape)

    u = _dot(xb, w_ref[:, 3 * aw:3 * aw + pw])
    ubuf[CARRY_ROWS:CARRY_ROWS + tm, :] = u
    pos = si * tm + lax.broadcasted_iota(jnp.int32, (tm, 1), 0)
    gd = pw // len(POOL_WINDOWS)
    for g, w in enumerate(POOL_WINDOWS):
        lo = g * gd
        ug = u[:, lo:lo + gd]
        acc = ug
        for back in range(1, w):
            acc = acc + ubuf[CARRY_ROWS - back:CARRY_ROWS - back + tm, lo:lo + gd]
        cnt = jnp.minimum(pos + 1, w).astype(F32)
        d = acc / cnt - ug
        out = _dot(d.astype(BF16), wp_ref[g])
        pool_ref[:, lo:lo + gd] = (out * ps_ref[:, lo:lo + gd]).astype(BF16)
    ubuf[0:CARRY_ROWS, :] = ubuf[tm:tm + CARRY_ROWS, :]

    @pl.when(si == n_s - 1)
    def _():
        ulast_ref[...] = u[tm - CARRY_ROWS:tm, :]


def _proj_prompt(x, w_main, wf_pad, bf_pad, wp, ps, *, tm):
    B, S, D = x.shape
    pw = ps.shape[1]
    aw = (w_main.shape[1] - pw) // 3
    scale = (aw // N_HEADS) ** -0.5
    n_s = S // tm
    row_blk = lambda width: pl.BlockSpec((None, tm, width), lambda b, s: (b, s, 0))
    const2 = lambda shape: pl.BlockSpec(shape, lambda b, s: (0,) * len(shape))
    out_shape = (
        jax.ShapeDtypeStruct((B, S, aw), F32),
        jax.ShapeDtypeStruct((B, S, aw), F32),
        jax.ShapeDtypeStruct((B, S, aw), BF16),
        jax.ShapeDtypeStruct((B, S, aw), BF16),
        jax.ShapeDtypeStruct((B, S, aw), BF16),
        jax.ShapeDtypeStruct((B, N_HEADS, S), F32),
        jax.ShapeDtypeStruct((B, N_HEADS, S), F32),
        jax.ShapeDtypeStruct((B, S, pw), BF16),
        jax.ShapeDtypeStruct((B, CARRY_ROWS, pw), F32),
    )
    head_blk = pl.BlockSpec((None, N_HEADS, tm), lambda b, s: (b, 0, s))
    out_specs = (row_blk(aw), row_blk(aw), row_blk(aw), row_blk(aw), row_blk(aw),
                 head_blk, head_blk, row_blk(pw),
                 pl.BlockSpec((None, CARRY_ROWS, pw), lambda b, s: (b, 0, 0)))
    return pl.pallas_call(
        functools.partial(_proj_kernel, tm=tm, aw=aw, pw=pw, scale=scale),
        grid=(B, n_s),
        in_specs=[row_blk(D), const2(w_main.shape), const2(wf_pad.shape), const2(bf_pad.shape),
                  const2(wp.shape), const2(ps.shape)],
        out_specs=out_specs,
        out_shape=out_shape,
        scratch_shapes=[pltpu.VMEM((CARRY_ROWS + tm, pw), F32), pltpu.VMEM((N_HEADS, LANES), F32)],
        compiler_params=pltpu.CompilerParams(
            dimension_semantics=("arbitrary", "arbitrary"), vmem_limit_bytes=VMEM_LIMIT),
        name="proj_prompt",
    )(x, w_main, wf_pad, bf_pad, wp, ps)


def _fox_kernel(q_ref, k_ref, v_ref, fr_ref, o_ref, m_ref, l_ref, acc_ref, *, tq, hd):
    qi = pl.program_id(2)
    q = q_ref[...]
    lane = lax.broadcasted_iota(jnp.int32, q.shape, 1)
    first = lane < hd
    zero = jnp.zeros_like(q)
    q_heads = (jnp.where(first, q, zero), jnp.where(first, zero, q))

    m_ref[...] = jnp.full(m_ref.shape, NEG_INF, F32)
    l_ref[...] = jnp.zeros(l_ref.shape, F32)
    acc_ref[...] = jnp.zeros(acc_ref.shape, F32)

    def step(j, masked):
        start = pl.multiple_of(j * tq, tq)
        kj = k_ref[pl.ds(start, tq), :]
        vj = v_ref[pl.ds(start, tq), :]
        alphas, pvs = [], []
        for h in range(2):
            s = _dot_nt(q_heads[h], kj) - fr_ref[h:h + 1, pl.ds(start, tq)]
            if masked:
                row = lax.broadcasted_iota(jnp.int32, s.shape, 0)
                col = lax.broadcasted_iota(jnp.int32, s.shape, 1)
                s = jnp.where(col <= row, s, NEG_INF)
            m_old = m_ref[h]
            m_new = jnp.maximum(m_old, jnp.max(s, axis=1, keepdims=True))
            p = jnp.exp(s - m_new)
            alpha = jnp.exp(m_old - m_new)
            l_ref[h] = alpha * l_ref[h] + jnp.sum(p, axis=1, keepdims=True)
            m_ref[h] = m_new
            alphas.append(alpha)
            pvs.append(_dot(p.astype(BF16), vj))
        alpha2 = jnp.where(first, alphas[0], alphas[1])
        acc_ref[...] = alpha2 * acc_ref[...] + jnp.where(first, pvs[0], pvs[1])

    def body(j, carry):
        step(j, False)
        return carry

    lax.fori_loop(0, qi, body, 0)
    step(qi, True)
    l2 = jnp.where(first, l_ref[0], l_ref[1])
    o_ref[...] = (acc_ref[...] / l2).astype(o_ref.dtype)


def _fox_prompt(qb, kb, vb, fr, *, tq):
    B, S, aw = qb.shape
    hd = aw // N_HEADS
    n_pairs = N_HEADS // 2
    fr4 = fr.reshape(B, n_pairs, 2, S)
    return pl.pallas_call(
        functools.partial(_fox_kernel, tq=tq, hd=hd),
        grid=(B, n_pairs, S // tq),
        in_specs=[
            pl.BlockSpec((None, tq, 2 * hd), lambda b, p, i: (b, i, p)),
            pl.BlockSpec((None, S, 2 * hd), lambda b, p, i: (b, 0, p)),
            pl.BlockSpec((None, S, 2 * hd), lambda b, p, i: (b, 0, p)),
            pl.BlockSpec((None, None, 2, S), lambda b, p, i: (b, p, 0, 0)),
        ],
        out_specs=pl.BlockSpec((None, tq, 2 * hd), lambda b, p, i: (b, i, p)),
        out_shape=jax.ShapeDtypeStruct((B, S, aw), BF16),
        scratch_shapes=[pltpu.VMEM((2, tq, 1), F32), pltpu.VMEM((2, tq, 1), F32),
                        pltpu.VMEM((tq, 2 * hd), F32)],
        compiler_params=pltpu.CompilerParams(
            dimension_semantics=("arbitrary", "arbitrary", "arbitrary"), vmem_limit_bytes=VMEM_LIMIT),
        name="fox_prompt",
    )(qb, kb, vb, fr4)


def _finish_kernel(x_ref, a_ref, p_ref, wo_ref, g1_ref, b1_ref, wg_ref, wu_ref, wd_ref, g2_ref, b2_ref,
                   o_ref, *, aw, alpha, ff_chunk):
    x = x_ref[...].astype(F32)
    mix = _dot(a_ref[...].astype(BF16), wo_ref[0:aw, :]) + _dot(p_ref[...].astype(BF16), wo_ref[aw:, :])
    x1 = _layer_norm(alpha * x + mix, g1_ref[...], b1_ref[...])
    x1b = x1.astype(BF16)
    d_ff = wg_ref.shape[1]
    ffn = None
    for c0 in range(0, d_ff, ff_chunk):
        c1 = min(c0 + ff_chunk, d_ff)
        gate = _dot(x1b, wg_ref[:, c0:c1])
        up = _dot(x1b, wu_ref[:, c0:c1])
        hid = (gate * jax.nn.sigmoid(gate) * up).astype(BF16)
        part = _dot(hid, wd_ref[c0:c1, :])
        ffn = part if ffn is None else ffn + part
    o_ref[...] = _layer_norm(alpha * x1 + ffn, g2_ref[...], b2_ref[...]).astype(o_ref.dtype)


def _finish(x2d, attn2d, pool2d, wo, g1, b1, wg, wu, wd, g2, b2, *, tm, alpha, ff_chunk=256):
    n, d = x2d.shape
    aw = attn2d.shape[1]
    row_blk = lambda width: pl.BlockSpec((tm, width), lambda i: (i, 0))
    const = lambda a: pl.BlockSpec(a.shape, lambda i: (0,) * a.ndim, pipeline_mode=pl.Buffered(1))
    consts = (wo, g1, b1, wg, wu, wd, g2, b2)
    return pl.pallas_call(
        functools.partial(_finish_kernel, aw=aw, alpha=alpha, ff_chunk=ff_chunk),
        grid=(n // tm,),
        in_specs=[row_blk(d), row_blk(aw), row_blk(pool2d.shape[1])] + [const(a) for a in consts],
        out_specs=row_blk(d),
        out_shape=jax.ShapeDtypeStruct((n, d), x2d.dtype),
        compiler_params=pltpu.CompilerParams(
            dimension_semantics=("arbitrary",), vmem_limit_bytes=VMEM_LIMIT),
        name="finish",
    )(x2d, attn2d, pool2d, *consts)


def _proj_sample_kernel(x_ref, w_ref, wf_ref, bf_ref, wp_ref, ps_ref, st_ref,
                        q_ref, k_ref, v_ref, lf_ref, pool_ref, nst_ref, *, aw, pw):
    xb = x_ref[...].astype(BF16)
    q_ref[...] = _dot(xb, w_ref[:, 0:aw])
    k_ref[...] = _dot(xb, w_ref[:, aw:2 * aw])
    v_ref[...] = _dot(xb, w_ref[:, 2 * aw:3 * aw])
    lf_ref[...] = _log_sigmoid(_dot(xb, wf_ref[...]) + bf_ref[...])
    u = _dot(xb, w_ref[:, 3 * aw:3 * aw + pw])
    gd = pw // len(POOL_WINDOWS)
    for g, w in enumerate(POOL_WINDOWS):
        lo = g * gd
        ug = u[:, lo:lo + gd]
        acc = ug
        for back in range(1, w):
            r = POOL_STATE_LEN - back
            acc = acc + st_ref[:, r * pw + lo:r * pw + lo + gd]
        d = acc / float(w) - ug
        out = _dot(d.astype(BF16), wp_ref[g])
        pool_ref[:, lo:lo + gd] = (out * ps_ref[:, lo:lo + gd]).astype(BF16)
    for r in range(POOL_STATE_LEN - 1):
        nst_ref[:, r * pw:(r + 1) * pw] = st_ref[:, (r + 1) * pw:(r + 2) * pw]
    nst_ref[:, (POOL_STATE_LEN - 1) * pw:] = u


def _proj_sample(xs2d, w_main, wf_pad, bf_pad, wp, ps, state2d):
    n = xs2d.shape[0]
    pw = ps.shape[1]
    aw = (w_main.shape[1] - pw) // 3
    out_shape = (
        jax.ShapeDtypeStruct((n, aw), F32), jax.ShapeDtypeStruct((n, aw), F32),
        jax.ShapeDtypeStruct((n, aw), F32), jax.ShapeDtypeStruct((n, LANES), F32),
        jax.ShapeDtypeStruct((n, pw), BF16), jax.ShapeDtypeStruct(state2d.shape, F32),
    )
    return pl.pallas_call(
        functools.partial(_proj_sample_kernel, aw=aw, pw=pw),
        out_shape=out_shape,
        compiler_params=pltpu.CompilerParams(vmem_limit_bytes=VMEM_LIMIT),
        name="proj_sample",
    )(xs2d, w_main, wf_pad, bf_pad, wp, ps, state2d)


def _decode_kernel(pt_ref, q_ref, kn_ref, vn_ref, lfn_ref, *rest, ppc, page, hd, scale):
    k_refs = rest[0:ppc]
    v_refs = rest[ppc:2 * ppc]
    lf_refs = rest[2 * ppc:3 * ppc]
    o_ref, m_ref, l_ref, acc_ref, g_ref = rest[3 * ppc:]
    c = pl.program_id(1)
    n_c = pl.num_programs(1)
    aw = q_ref.shape[-1]

    @pl.when(c == 0)
    def _():
        m_ref[...] = jnp.full(m_ref.shape, NEG_INF, F32)
        l_ref[...] = jnp.zeros(l_ref.shape, F32)
        acc_ref[...] = jnp.zeros(acc_ref.shape, F32)
        g_ref[...] = jnp.zeros(g_ref.shape, F32)

    head_of_lane = lax.broadcasted_iota(jnp.int32, (N_HEADS, aw), 1) // hd
    head_row = lax.broadcasted_iota(jnp.int32, (N_HEADS, aw), 0)
    own = head_of_lane == head_row
    qbd = jnp.where(own, jnp.broadcast_to(q_ref[...] * scale, (N_HEADS, aw)), 0.0).astype(BF16)

    g = g_ref[:, 0:1]
    s_parts = []
    for i in range(ppc):
        s = _dot_nt(qbd, k_refs[i][...].astype(BF16))
        cs = _lane_cumsum(lf_refs[i][...]) + g
        g = cs[:, page - 1:page]
        s_parts.append(s - cs)
    s_all = jnp.concatenate(s_parts, axis=1)
    m_old = m_ref[:, 0:1]
    m_new = jnp.maximum(m_old, jnp.max(s_all, axis=1, keepdims=True))
    p = jnp.exp(s_all - m_new)
    alpha = jnp.exp(m_old - m_new)
    l_new = alpha * l_ref[:, 0:1] + jnp.sum(p, axis=1, keepdims=True)
    pb = p.astype(BF16)
    pv = _dot(pb[:, 0:page], v_refs[0][...].astype(BF16))
    for i in range(1, ppc):
        pv = pv + _dot(pb[:, i * page:(i + 1) * page], v_refs[i][...].astype(BF16))
    acc_new = alpha * acc_ref[...] + pv
    m_ref[...] = jnp.broadcast_to(m_new, m_ref.shape)
    l_ref[...] = jnp.broadcast_to(l_new, l_ref.shape)
    acc_ref[...] = acc_new
    g_ref[...] = jnp.broadcast_to(g, g_ref.shape)

    @pl.when(c == n_c - 1)
    def _():
        g_new = g + lfn_ref[...]
        kn = kn_ref[...].astype(BF16).astype(F32)
        vn = vn_ref[...].astype(BF16).astype(F32)
        s_n = jnp.sum(qbd.astype(F32) * kn, axis=1, keepdims=True) - g_new
        m_fin = jnp.maximum(m_new, s_n)
        a_fin = jnp.exp(m_new - m_fin)
        p_n = jnp.exp(s_n - m_fin)
        l_fin = a_fin * l_new + p_n
        acc_fin = a_fin * acc_new + p_n.astype(BF16).astype(F32) * vn
        out = jnp.where(own, acc_fin / l_fin, 0.0)
        o_ref[...] = jnp.sum(out, axis=0, keepdims=True)


def _fox_sample(page_table, q, k_new, v_new, lf_new, cache_k, cache_v, cache_lft, *, ppc):
    n, aw = q.shape
    n_pages = page_table.shape[1]
    page = cache_k.shape[1]
    hd = aw // N_HEADS
    row3 = lambda a: a.reshape(n, 1, a.shape[-1])
    row_spec = pl.BlockSpec((None, 1, aw), lambda b, c, pt: (b, 0, 0))

    def page_spec(shape, i):
        return pl.BlockSpec((None,) + shape, lambda b, c, pt: (pt[b, c * ppc + i], 0, 0))

    in_specs = [row_spec, row_spec, row_spec,
                pl.BlockSpec((None, N_HEADS, 1), lambda b, c, pt: (b, 0, 0))]
    in_specs += [page_spec((page, aw), i) for i in range(ppc)]
    in_specs += [page_spec((page, aw), i) for i in range(ppc)]
    in_specs += [page_spec((N_HEADS, page), i) for i in range(ppc)]
    grid_spec = pltpu.PrefetchScalarGridSpec(
        num_scalar_prefetch=1,
        grid=(n, n_pages // ppc),
        in_specs=in_specs,
        out_specs=pl.BlockSpec((None, 1, aw), lambda b, c, pt: (b, 0, 0)),
        scratch_shapes=[pltpu.VMEM((N_HEADS, LANES), F32), pltpu.VMEM((N_HEADS, LANES), F32),
                        pltpu.VMEM((N_HEADS, aw), F32), pltpu.VMEM((N_HEADS, LANES), F32)],
    )
    out = pl.pallas_call(
        functools.partial(_decode_kernel, ppc=ppc, page=page, hd=hd, scale=hd ** -0.5),
        grid_spec=grid_spec,
        out_shape=jax.ShapeDtypeStruct((n, 1, aw), F32),
        compiler_params=pltpu.CompilerParams(
            dimension_semantics=("arbitrary", "arbitrary"), vmem_limit_bytes=VMEM_LIMIT),
        name="fox_sample",
    )(page_table, row3(q), row3(k_new), row3(v_new), lf_new.reshape(n, N_HEADS, 1),
      *([cache_k] * ppc), *([cache_v] * ppc), *([cache_lft] * ppc))
    return out.reshape(n, aw)


def kernel(x_prompt, x_sample, cache_k, cache_v, cache_logf, state_pool, page_table,
           w_in, b_f, w_pool, pool_scale, w_o, ln1_g, ln1_b, w_gate, w_up, w_down, ln2_g, ln2_b):
    depth = w_in.shape[0]
    assert depth == 1
    B, S, D = x_prompt.shape
    DB, DS, _ = x_sample.shape
    assert DS == 1
    aw = D // 2
    pw = D - aw
    hd = aw // N_HEADS
    alpha = (2.0 * depth) ** 0.25
    n_pool, page = cache_k.shape[1], cache_k.shape[2]

    l = 0
    w_main = jnp.concatenate([w_in[l][:, :3 * aw], w_in[l][:, 3 * aw + N_HEADS:]], axis=1).astype(BF16)
    wf_pad = jnp.pad(w_in[l][:, 3 * aw:3 * aw + N_HEADS], ((0, 0), (0, LANES - N_HEADS))).astype(BF16)
    bf_pad = jnp.pad(b_f[l], (0, LANES - N_HEADS)).reshape(1, LANES)
    wp = w_pool[l].astype(BF16)
    ps = pool_scale[l].reshape(1, pw)
    wo = w_o[l].astype(BF16)
    wg, wu, wd = w_gate[l].astype(BF16), w_up[l].astype(BF16), w_down[l].astype(BF16)
    g1, b1 = ln1_g[l].reshape(1, D), ln1_b[l].reshape(1, D)
    g2, b2 = ln2_g[l].reshape(1, D), ln2_b[l].reshape(1, D)

    k32, v32, qb, kb, vb, lft, fr, pool_p, ulast = _proj_prompt(
        x_prompt, w_main, wf_pad, bf_pad, wp, ps, tm=512)
    attn_p = _fox_prompt(qb, kb, vb, fr, tq=512)
    y_prompt = _finish(x_prompt.reshape(B * S, D), attn_p.reshape(B * S, aw), pool_p.reshape(B * S, pw),
                       wo, g1, b1, wg, wu, wd, g2, b2, tm=512, alpha=alpha).reshape(B, S, D)

    state2d = state_pool[l].reshape(DB, POOL_STATE_LEN * pw)
    q_s, k_s, v_s, lf_s, pool_s, nstate = _proj_sample(
        x_sample.reshape(DB, D), w_main, wf_pad, bf_pad, wp, ps, state2d)
    lf_s8 = lf_s[:, :N_HEADS]
    cache_lft = jnp.swapaxes(cache_logf[l], 1, 2)
    attn_s = _fox_sample(page_table, q_s, k_s, v_s, lf_s8,
                         cache_k[l].reshape(n_pool, page, aw), cache_v[l].reshape(n_pool, page, aw),
                         cache_lft, ppc=8)
    y_sample = _finish(x_sample.reshape(DB, D), attn_s, pool_s, wo, g1, b1, wg, wu, wd, g2, b2,
                       tm=DB, alpha=alpha).reshape(DB, DS, D)

    return (y_prompt, y_sample,
            k32.reshape(1, B, S, N_HEADS, hd), v32.reshape(1, B, S, N_HEADS, hd),
            jnp.swapaxes(lft, 1, 2)[None],
            ulast[:, CARRY_ROWS - POOL_STATE_LEN:, :][None],
            k_s.reshape(1, DB, DS, N_HEADS, hd), v_s.reshape(1, DB, DS, N_HEADS, hd),
            lf_s8.reshape(1, DB, DS, N_HEADS),
            nstate.reshape(1, DB, POOL_STATE_LEN, pw))
```

```python
import functools
import math

import jax
import jax.numpy as jnp
from jax import lax
from jax.experimental import pallas as pl
from jax.experimental.pallas import tpu as pltpu

N_HEADS = 8
POOL_WINDOWS = (2, 4, 8, 16)
POOL_STATE_LEN = max(POOL_WINDOWS) - 1
LN_EPS = 1e-5
NEG_INF = -1e30
LOG2E = math.log2(math.e)

LANES = 128
CARRY_ROWS = 16
VMEM_LIMIT = 56 * 1024 * 1024

EXT_LANE0 = N_HEADS
EXT_SPLIT = 3

BF16 = jnp.bfloat16
F32 = jnp.float32


def _dot(a, b):
    return jnp.dot(a, b, preferred_element_type=F32)


def _dot_nt(a, b):
    return lax.dot_general(a, b, (((1,), (1,)), ((), ())), preferred_element_type=F32)


def _log_sigmoid(z):
    return jnp.minimum(z, 0.0) - jnp.log1p(jnp.exp(-jnp.abs(z)))


def _cumsum(x, axis):
    n = x.shape[axis]
    idx = lax.broadcasted_iota(jnp.int32, x.shape, axis)
    shift = 1
    while shift < n:
        x = x + jnp.where(idx >= shift, pltpu.roll(x, shift, axis=axis), 0.0)
        shift *= 2
    return x


def _layer_norm(x, g, b):
    mu = jnp.mean(x, axis=-1, keepdims=True)
    xc = x - mu
    var = jnp.mean(xc * xc, axis=-1, keepdims=True)
    return xc * lax.rsqrt(var + LN_EPS) * g + b


def _pool_mix(u, window_row, cnt_of, wp_ref, ps_ref, pool_ref):
    gd = u.shape[1] // len(POOL_WINDOWS)
    for g, w in enumerate(POOL_WINDOWS):
        lo = g * gd
        ug = u[:, lo:lo + gd]
        acc = ug
        for back in range(1, w):
            acc = acc + window_row(back, lo, gd)
        d = acc / cnt_of(w) - ug
        out = _dot(d.astype(BF16), wp_ref[g])
        pool_ref[:, lo:lo + gd] = (out * ps_ref[:, lo:lo + gd]).astype(BF16)


def _proj_kernel(x_ref, w_ref, wvt_ref, wf_ref, bf_ref, piece_ref, wp_ref, ps_ref,
                 k32_ref, vt32_ref, qb_ref, kb_ref, kext_ref, vtb_ref, lft_ref, pool_ref, ulast_ref,
                 ubuf, fcarry, *, tm, aw, pw, qscale):
    si = pl.program_id(1)
    n_s = pl.num_programs(1)

    @pl.when(si == 0)
    def _():
        ubuf[0:CARRY_ROWS, :] = jnp.zeros((CARRY_ROWS, pw), F32)
        fcarry[...] = jnp.zeros(fcarry.shape, F32)

    xb = x_ref[...].astype(BF16)
    q = _dot(xb, w_ref[:, 0:aw])
    qb_ref[...] = (q * qscale).astype(BF16)
    k = _dot(xb, w_ref[:, aw:2 * aw])
    k32_ref[...] = k
    kb_ref[...] = k.astype(BF16)
    vt = _dot_nt(wvt_ref[...], xb)
    vt32_ref[...] = vt
    vtb_ref[...] = vt.astype(BF16)

    lane = lax.broadcasted_iota(jnp.int32, (tm, LANES), 1)
    used = lane < EXT_LANE0 + EXT_SPLIT * N_HEADS
    lf = jnp.where(used, _log_sigmoid(_dot(xb, wf_ref[...]) + bf_ref[...]), 0.0)
    lft_ref[...] = jnp.transpose(lf)[0:N_HEADS, :]
    f2 = (_cumsum(lf, 0) + fcarry[0:1, :])
    fcarry[...] = jnp.broadcast_to(f2[tm - 1:tm, :], fcarry.shape)
    f2 = f2 * LOG2E
    hi = f2.astype(BF16)
    r1 = f2 - hi.astype(F32)
    mid = r1.astype(BF16)
    lo = (r1 - mid.astype(F32)).astype(BF16)
    piece = piece_ref[...]
    ext = jnp.where(piece == 0, hi, jnp.where(piece == 1, mid, lo))
    kext_ref[...] = jnp.where(piece >= 0, ext, jnp.zeros_like(ext))

    u = _dot(xb, w_ref[:, 2 * aw:2 * aw + pw])
    ubuf[CARRY_ROWS:CARRY_ROWS + tm, :] = u
    pos = si * tm + lax.broadcasted_iota(jnp.int32, (tm, 1), 0)
    _pool_mix(u,
              lambda back, lo_, gd: ubuf[CARRY_ROWS - back:CARRY_ROWS - back + tm, lo_:lo_ + gd],
              lambda w: jnp.minimum(pos + 1, w).astype(F32),
              wp_ref, ps_ref, pool_ref)
    ubuf[0:CARRY_ROWS, :] = ubuf[tm:tm + CARRY_ROWS, :]

    @pl.when(si == n_s - 1)
    def _():
        ulast_ref[...] = u[tm - CARRY_ROWS:tm, :]


def _proj_prompt(x, w_qku, w_vt, wf_ext, bf_ext, wp, ps, *, tm):
    B, S, D = x.shape
    pw = ps.shape[1]
    aw = w_vt.shape[0]
    hd = aw // N_HEADS
    n_s = S // tm
    row_blk = lambda width: pl.BlockSpec((None, tm, width), lambda b, s: (b, s, 0))
    col_blk = lambda rows: pl.BlockSpec((None, rows, tm), lambda b, s: (b, 0, s))
    const = lambda a: pl.BlockSpec(a.shape, lambda b, s: (0,) * a.ndim)
    out_shape = (
        jax.ShapeDtypeStruct((B, S, aw), F32),
        jax.ShapeDtypeStruct((B, aw, S), F32),
        jax.ShapeDtypeStruct((B, S, aw), BF16),
        jax.ShapeDtypeStruct((B, S, aw), BF16),
        jax.ShapeDtypeStruct((B, S, LANES), BF16),
        jax.ShapeDtypeStruct((B, aw, S), BF16),
        jax.ShapeDtypeStruct((B, N_HEADS, S), F32),
        jax.ShapeDtypeStruct((B, S, pw), BF16),
        jax.ShapeDtypeStruct((B, CARRY_ROWS, pw), F32),
    )
    out_specs = (row_blk(aw), col_blk(aw), row_blk(aw), row_blk(aw), row_blk(LANES), col_blk(aw),
                 col_blk(N_HEADS), row_blk(pw),
                 pl.BlockSpec((None, CARRY_ROWS, pw), lambda b, s: (b, 0, 0)))
    lane = jnp.arange(LANES, dtype=jnp.int32)
    split = (lane >= EXT_LANE0) & (lane < EXT_LANE0 + EXT_SPLIT * N_HEADS)
    piece = jnp.where(split, (lane - EXT_LANE0) % EXT_SPLIT, -1).reshape(1, LANES)
    consts = (w_qku, w_vt, wf_ext, bf_ext, piece, wp, ps)
    return pl.pallas_call(
        functools.partial(_proj_kernel, tm=tm, aw=aw, pw=pw, qscale=LOG2E * hd ** -0.5),
        grid=(B, n_s),
        in_specs=[row_blk(D)] + [const(a) for a in consts],
        out_specs=out_specs,
        out_shape=out_shape,
        scratch_shapes=[pltpu.VMEM((CARRY_ROWS + tm, pw), F32), pltpu.VMEM((8, LANES), F32)],
        compiler_params=pltpu.CompilerParams(
            dimension_semantics=("arbitrary", "arbitrary"), vmem_limit_bytes=VMEM_LIMIT),
        name="proj_prompt",
    )(x, *consts)


def _fox_kernel(q_ref, k_ref, kext_ref, vt_ref, o_ref, m_ref, l_ref, acc_ref, *, tq, tk, hd):
    pair = pl.program_id(1)
    qi = pl.program_id(2)
    q = q_ref[...].astype(F32)
    lane = lax.broadcasted_iota(jnp.int32, q.shape, 1)

    def between(lo, hi, val):
        return jnp.where((lane >= lo) & (lane < hi), val, 0.0)

    q_aug = []
    for h in range(2):
        e0 = EXT_LANE0 + EXT_SPLIT * (2 * pair + h)
        q_aug.append(jnp.concatenate([between(h * hd, (h + 1) * hd, q), between(e0, e0 + EXT_SPLIT, -1.0)],
                                     axis=1).astype(BF16))

    m_ref[...] = jnp.full(m_ref.shape, NEG_INF, F32)
    l_ref[...] = jnp.zeros(l_ref.shape, F32)
    acc_ref[...] = jnp.zeros(acc_ref.shape, F32)

    def step(j, masked):
        start = pl.multiple_of(j * tk, tk)
        kk = jnp.concatenate([k_ref[pl.ds(start, tk), :], kext_ref[pl.ds(start, tk), :]], axis=1)
        vt = vt_ref[:, pl.ds(start, tk)]
        sts = [_dot_nt(kk, q_aug[h]) for h in range(2)]
        for h in range(2):
            st = sts[h]
            if masked:
                key = start + lax.broadcasted_iota(jnp.int32, st.shape, 0)
                qry = qi * tq + lax.broadcasted_iota(jnp.int32, st.shape, 1)
                st = jnp.where(key <= qry, st, NEG_INF)
            m_old = m_ref[h]
            m_new = jnp.maximum(m_old, jnp.max(st, axis=0, keepdims=True))
            pt = jnp.exp2(st - m_new)
            alpha = jnp.exp2(m_old - m_new)
            l_ref[h] = alpha * l_ref[h] + jnp.sum(pt, axis=0, keepdims=True)
            m_ref[h] = m_new
            rows = slice(h * hd, (h + 1) * hd)
            acc_ref[rows, :] = alpha * acc_ref[rows, :] + _dot(vt[rows, :], pt.astype(BF16))

    def body(j, carry):
        step(j, False)
        return carry

    n_diag = tq // tk
    lax.fori_loop(0, qi * n_diag, body, 0)
    for t in range(n_diag):
        step(qi * n_diag + t, True)
    out_t = jnp.concatenate([acc_ref[0:hd, :] / l_ref[0], acc_ref[hd:2 * hd, :] / l_ref[1]], axis=0)
    o_ref[...] = jnp.transpose(out_t).astype(o_ref.dtype)


def _fox_prompt(qb, kb, kext, vtb, *, tq, tk):
    B, S, aw = qb.shape
    hd = aw // N_HEADS
    n_pairs = N_HEADS // 2
    return pl.pallas_call(
        functools.partial(_fox_kernel, tq=tq, tk=tk, hd=hd),
        grid=(B, n_pairs, S // tq),
        in_specs=[
            pl.BlockSpec((None, tq, 2 * hd), lambda b, p, i: (b, i, p)),
            pl.BlockSpec((None, S, 2 * hd), lambda b, p, i: (b, 0, p)),
            pl.BlockSpec((None, S, LANES), lambda b, p, i: (b, 0, 0)),
            pl.BlockSpec((None, 2 * hd, S), lambda b, p, i: (b, p, 0)),
        ],
        out_specs=pl.BlockSpec((None, tq, 2 * hd), lambda b, p, i: (b, i, p)),
        out_shape=jax.ShapeDtypeStruct((B, S, aw), BF16),
        scratch_shapes=[pltpu.VMEM((2, 1, tq), F32), pltpu.VMEM((2, 1, tq), F32),
                        pltpu.VMEM((2 * hd, tq), F32)],
        compiler_params=pltpu.CompilerParams(
            dimension_semantics=("arbitrary", "arbitrary", "arbitrary"), vmem_limit_bytes=VMEM_LIMIT),
        name="fox_prompt",
    )(qb, kb, kext, vtb)


def _finish_kernel(x_ref, a_ref, p_ref, wo_ref, g1_ref, b1_ref, wg_ref, wu_ref, wd_ref, g2_ref, b2_ref,
                   o_ref, *, aw, alpha, ff_chunk):
    x = x_ref[...].astype(F32)
    mix = _dot(a_ref[...].astype(BF16), wo_ref[0:aw, :]) + _dot(p_ref[...].astype(BF16), wo_ref[aw:, :])
    x1 = _layer_norm(alpha * x + mix, g1_ref[...], b1_ref[...])
    x1b = x1.astype(BF16)
    d_ff = wg_ref.shape[1]
    ffn = None
    for c0 in range(0, d_ff, ff_chunk):
        c1 = min(c0 + ff_chunk, d_ff)
        gate = _dot(x1b, wg_ref[:, c0:c1])
        up = _dot(x1b, wu_ref[:, c0:c1])
        hid = (gate * jax.nn.sigmoid(gate) * up).astype(BF16)
        part = _dot(hid, wd_ref[c0:c1, :])
        ffn = part if ffn is None else ffn + part
    o_ref[...] = _layer_norm(alpha * x1 + ffn, g2_ref[...], b2_ref[...]).astype(o_ref.dtype)


def _finish(x2d, attn2d, pool2d, wo, g1, b1, wg, wu, wd, g2, b2, *, tm, alpha, ff_chunk=256):
    n, d = x2d.shape
    aw = attn2d.shape[1]
    row_blk = lambda width: pl.BlockSpec((tm, width), lambda i: (i, 0))
    const = lambda a: pl.BlockSpec(a.shape, lambda i: (0,) * a.ndim, pipeline_mode=pl.Buffered(1))
    consts = (wo, g1, b1, wg, wu, wd, g2, b2)
    return pl.pallas_call(
        functools.partial(_finish_kernel, aw=aw, alpha=alpha, ff_chunk=ff_chunk),
        grid=(n // tm,),
        in_specs=[row_blk(d), row_blk(aw), row_blk(pool2d.shape[1])] + [const(a) for a in consts],
        out_specs=row_blk(d),
        out_shape=jax.ShapeDtypeStruct((n, d), x2d.dtype),
        compiler_params=pltpu.CompilerParams(
            dimension_semantics=("arbitrary",), vmem_limit_bytes=VMEM_LIMIT),
        name="finish",
    )(x2d, attn2d, pool2d, *consts)


def _proj_sample_kernel(x_ref, w_ref, wvt_ref, wf_ref, bf_ref, wp_ref, ps_ref, st_ref,
                        q_ref, k_ref, v_ref, lf_ref, pool_ref, nst_ref, *, aw, pw):
    xb = x_ref[...].astype(BF16)
    q_ref[...] = _dot(xb, w_ref[:, 0:aw])
    k_ref[...] = _dot(xb, w_ref[:, aw:2 * aw])
    v_ref[...] = _dot_nt(xb, wvt_ref[...])
    lf_ref[...] = _log_sigmoid(_dot(xb, wf_ref[...]) + bf_ref[...])
    u = _dot(xb, w_ref[:, 2 * aw:2 * aw + pw])
    _pool_mix(u,
              lambda back, lo, gd: st_ref[:, (POOL_STATE_LEN - back) * pw + lo:(POOL_STATE_LEN - back) * pw + lo + gd],
              lambda w: float(w),
              wp_ref, ps_ref, pool_ref)
    for r in range(POOL_STATE_LEN - 1):
        nst_ref[:, r * pw:(r + 1) * pw] = st_ref[:, (r + 1) * pw:(r + 2) * pw]
    nst_ref[:, (POOL_STATE_LEN - 1) * pw:] = u


def _proj_sample(xs2d, w_qku, w_vt, wf_ext, bf_ext, wp, ps, state2d):
    n = xs2d.shape[0]
    pw = ps.shape[1]
    aw = w_vt.shape[0]
    out_shape = (
        jax.ShapeDtypeStruct((n, aw), F32), jax.ShapeDtypeStruct((n, aw), F32),
        jax.ShapeDtypeStruct((n, aw), F32), jax.ShapeDtypeStruct((n, LANES), F32),
        jax.ShapeDtypeStruct((n, pw), BF16), jax.ShapeDtypeStruct(state2d.shape, F32),
    )
    return pl.pallas_call(
        functools.partial(_proj_sample_kernel, aw=aw, pw=pw),
        out_shape=out_shape,
        compiler_params=pltpu.CompilerParams(vmem_limit_bytes=VMEM_LIMIT),
        name="proj_sample",
    )(xs2d, w_qku, w_vt, wf_ext, bf_ext, wp, ps, state2d)


def _decode_kernel(pt_ref, q_ref, kn_ref, vn_ref, lfn_ref, *rest, ppc, page, scale):
    k_refs = rest[0:ppc]
    v_refs = rest[ppc:2 * ppc]
    lf_refs = rest[2 * ppc:3 * ppc]
    o_ref, qb_ref, m_ref, l_ref, acc_ref, g_ref = rest[3 * ppc:]
    c = pl.program_id(1)
    n_c = pl.num_programs(1)

    @pl.when(c == 0)
    def _():
        qb_ref[...] = jnp.broadcast_to(q_ref[...] * scale, qb_ref.shape)
        m_ref[...] = jnp.full(m_ref.shape, NEG_INF, F32)
        l_ref[...] = jnp.zeros(l_ref.shape, F32)
        acc_ref[...] = jnp.zeros(acc_ref.shape, F32)
        g_ref[...] = jnp.zeros(g_ref.shape, F32)

    def heads_leading(a):
        return jnp.stack([a[h:h + 1, :] for h in range(N_HEADS)], axis=0)

    lf_all = jnp.concatenate([r[...] for r in lf_refs], axis=0)
    src = lax.broadcasted_iota(jnp.int32, (page, 2 * page), 0)
    dst = lax.broadcasted_iota(jnp.int32, (page, 2 * page), 1)
    scan_w = jnp.where((src <= dst) | (dst >= page), 1.0, 0.0).astype(BF16)
    hi = lf_all.astype(BF16)
    r1 = lf_all - hi.astype(F32)
    mid = r1.astype(BF16)
    lo = (r1 - mid.astype(F32)).astype(BF16)
    scan = _dot(hi, scan_w) + _dot(mid, scan_w) + _dot(lo, scan_w)

    off = g_ref[...]
    s_parts = []
    for i in range(ppc):
        rows = slice(i * N_HEADS, (i + 1) * N_HEADS)
        s = jnp.sum(k_refs[i][...] * qb_ref[...], axis=1)
        s_parts.append(s - (scan[rows, 0:page] + off))
        off = off + scan[rows, page:2 * page]
    g_ref[...] = off
    s_all = jnp.concatenate(s_parts, axis=1)
    m_old = m_ref[:, 0:1]
    m_new = jnp.maximum(m_old, jnp.max(s_all, axis=1, keepdims=True))
    p = jnp.exp(s_all - m_new)
    alpha = jnp.exp(m_old - m_new)
    l_new = alpha * l_ref[:, 0:1] + jnp.sum(p, axis=1, keepdims=True)
    m_ref[...] = jnp.broadcast_to(m_new, m_ref.shape)
    l_ref[...] = jnp.broadcast_to(l_new, l_ref.shape)
    for h in range(N_HEADS):
        a = alpha[h:h + 1, :] * acc_ref[h]
        for i in range(ppc):
            a = a + p[h:h + 1, i * page:(i + 1) * page] * v_refs[i][h]
        acc_ref[h] = a

    @pl.when(c == n_c - 1)
    def _():
        g_new = heads_leading(off[:, 0:1]) + lfn_ref[...]
        s_n = jnp.sum(qb_ref[:, :, 0:1] * kn_ref[...], axis=1, keepdims=True) - g_new
        m_prev = heads_leading(m_new)
        m_fin = jnp.maximum(m_prev, s_n)
        a_fin = jnp.exp(m_prev - m_fin)
        p_n = jnp.exp(s_n - m_fin)
        l_fin = a_fin * heads_leading(l_new) + p_n
        past = jnp.sum(acc_ref[...], axis=2, keepdims=True)
        o_ref[...] = (a_fin * past + p_n * vn_ref[...]) / l_fin


def _fox_sample(page_table, q, k_new, v_new, lf_new, cache_kt, cache_vt, cache_lft, *, ppc):
    n, aw = q.shape
    n_pages = page_table.shape[1]
    hd, page = cache_kt.shape[2], cache_kt.shape[3]
    col = lambda a: a.reshape(n, N_HEADS, hd, 1)
    col_spec = pl.BlockSpec((None, N_HEADS, hd, 1), lambda b, c, pt: (b, 0, 0, 0))

    def page_spec(shape, i):
        return pl.BlockSpec((None,) + shape, lambda b, c, pt: (pt[b, c * ppc + i],) + (0,) * len(shape))

    in_specs = [col_spec, col_spec, col_spec,
                pl.BlockSpec((None, N_HEADS, 1, 1), lambda b, c, pt: (b, 0, 0, 0))]
    in_specs += [page_spec((N_HEADS, hd, page), i) for i in range(ppc)]
    in_specs += [page_spec((N_HEADS, hd, page), i) for i in range(ppc)]
    in_specs += [page_spec((N_HEADS, page), i) for i in range(ppc)]
    grid_spec = pltpu.PrefetchScalarGridSpec(
        num_scalar_prefetch=1,
        grid=(n, n_pages // ppc),
        in_specs=in_specs,
        out_specs=col_spec,
        scratch_shapes=[pltpu.VMEM((N_HEADS, hd, page), F32),
                        pltpu.VMEM((N_HEADS, LANES), F32), pltpu.VMEM((N_HEADS, LANES), F32),
                        pltpu.VMEM((N_HEADS, hd, page), F32), pltpu.VMEM((N_HEADS, LANES), F32)],
    )
    out = pl.pallas_call(
        functools.partial(_decode_kernel, ppc=ppc, page=page, scale=hd ** -0.5),
        grid_spec=grid_spec,
        out_shape=jax.ShapeDtypeStruct((n, N_HEADS, hd, 1), F32),
        compiler_params=pltpu.CompilerParams(
            dimension_semantics=("arbitrary", "arbitrary"), vmem_limit_bytes=VMEM_LIMIT),
        name="fox_sample",
    )(page_table, col(q), col(k_new), col(v_new), lf_new.reshape(n, N_HEADS, 1, 1),
      *([cache_kt] * ppc), *([cache_vt] * ppc), *([cache_lft] * ppc))
    return out.reshape(n, aw)


def kernel(x_prompt, x_sample, cache_k, cache_v, cache_logf, state_pool, page_table,
           w_in, b_f, w_pool, pool_scale, w_o, ln1_g, ln1_b, w_gate, w_up, w_down, ln2_g, ln2_b):
    depth = w_in.shape[0]
    assert depth == 1
    B, S, D = x_prompt.shape
    DB, DS, _ = x_sample.shape
    assert DS == 1
    aw = D // 2
    pw = D - aw
    hd = aw // N_HEADS
    alpha = (2.0 * depth) ** 0.25

    l = 0
    w = w_in[l]
    w_qku = jnp.concatenate([w[:, :2 * aw], w[:, 3 * aw + N_HEADS:]], axis=1).astype(BF16)
    w_vt = w[:, 2 * aw:3 * aw].T.astype(BF16)
    w_f = w[:, 3 * aw:3 * aw + N_HEADS]
    pad = LANES - EXT_LANE0 - EXT_SPLIT * N_HEADS
    wf_ext = jnp.pad(jnp.concatenate([w_f, jnp.repeat(w_f, EXT_SPLIT, axis=1)], axis=1),
                     ((0, 0), (0, pad))).astype(BF16)
    bf_ext = jnp.pad(jnp.concatenate([b_f[l], jnp.repeat(b_f[l], EXT_SPLIT)]), (0, pad)).reshape(1, LANES)
    wp = w_pool[l].astype(BF16)
    ps = pool_scale[l].reshape(1, pw)
    wo = w_o[l].astype(BF16)
    wg, wu, wd = w_gate[l].astype(BF16), w_up[l].astype(BF16), w_down[l].astype(BF16)
    g1, b1 = ln1_g[l].reshape(1, D), ln1_b[l].reshape(1, D)
    g2, b2 = ln2_g[l].reshape(1, D), ln2_b[l].reshape(1, D)

    k32, vt32, qb, kb, kext, vtb, lft, pool_p, ulast = _proj_prompt(
        x_prompt, w_qku, w_vt, wf_ext, bf_ext, wp, ps, tm=512)
    attn_p = _fox_prompt(qb, kb, kext, vtb, tq=512, tk=512)
    y_prompt = _finish(x_prompt.reshape(B * S, D), attn_p.reshape(B * S, aw), pool_p.reshape(B * S, pw),
                       wo, g1, b1, wg, wu, wd, g2, b2, tm=512, alpha=alpha).reshape(B, S, D)

    state2d = state_pool[l].reshape(DB, POOL_STATE_LEN * pw)
    q_s, k_s, v_s, lf_s, pool_s, nstate = _proj_sample(
        x_sample.reshape(DB, D), w_qku, w_vt, wf_ext, bf_ext, wp, ps, state2d)
    lf_s8 = lf_s[:, :N_HEADS]
    attn_s = _fox_sample(page_table, q_s, k_s, v_s, lf_s8,
                         jnp.transpose(cache_k[l], (0, 2, 3, 1)), jnp.transpose(cache_v[l], (0, 2, 3, 1)),
                         jnp.swapaxes(cache_logf[l], 1, 2), ppc=16)
    y_sample = _finish(x_sample.reshape(DB, D), attn_s, pool_s, wo, g1, b1, wg, wu, wd, g2, b2,
                       tm=DB, alpha=alpha).reshape(DB, DS, D)

    v_prompt = jnp.transpose(vt32.reshape(B, N_HEADS, hd, S), (0, 3, 1, 2))[None]
    return (y_prompt, y_sample,
            k32.reshape(1, B, S, N_HEADS, hd), v_prompt,
            jnp.swapaxes(lft, 1, 2)[None],
            ulast[:, CARRY_ROWS - POOL_STATE_LEN:, :][None],
            k_s.reshape(1, DB, DS, N_HEADS, hd), v_s.reshape(1, DB, DS, N_HEADS, hd),
            lf_s8.reshape(1, DB, DS, N_HEADS),
            nstate.reshape(1, DB, POOL_STATE_LEN, pw))
```
